```python
import jax, jax.numpy as jnp
from jax import lax
import numpy as np

D_MODEL = 2048
BATCH = 2
SEQ = 16384
DEPTH = 2

CONV_WIDTH = D_MODEL // 2
CONV_GROUPS = 8
CONV_K = 3
SGU_WIDTH = D_MODEL // 2
SGU_GROUPS = 8
SGU_GROUP_DIM = SGU_WIDTH // SGU_GROUPS
SGU_CHUNK = 128
MIX_IN_WIDTH = 3 * CONV_WIDTH + 2 * SGU_WIDTH
N_HEADS = 16
HEAD_DIM = D_MODEL // N_HEADS
MOBA_BLOCK = 256
MOBA_TOPK = 3
Q_CHUNK = 32
D_FF = 4 * D_MODEL
N_EVEN = (DEPTH + 1) // 2
N_ODD = DEPTH // 2
EPS = 1e-6

kernel_name = "hybrid_conv_sgu_moba_block"


def rms_norm(x, g):
    xf = x.astype(jnp.float32)
    y = xf * lax.rsqrt(jnp.mean(xf * xf, axis=-1, keepdims=True) + EPS)
    return (y * g.astype(jnp.float32)).astype(x.dtype)


def alibi_slopes():
    return 2.0 ** (-8.0 * jnp.arange(1, N_HEADS + 1, dtype=jnp.float32) / N_HEADS)


def conv_sgu_mixer(h, w_in, conv_w, sgu_gain, w_s, b_s, w_out):
    B_, S, _ = h.shape
    z = h @ w_in
    gate_b, gate_c, xa, u, v = jnp.split(
        z, [CONV_WIDTH, 2 * CONV_WIDTH, 3 * CONV_WIDTH, 3 * CONV_WIDTH + SGU_WIDTH], axis=-1)
    y = gate_c * xa
    yp = jnp.pad(y, ((0, 0), (CONV_K - 1, 0), (0, 0)))
    conv = yp[:, 0:S, :] * conv_w[:, 0]
    for tap in range(1, CONV_K):
        conv = conv + yp[:, tap:tap + S, :] * conv_w[:, tap]
    a_out = gate_b * conv
    u = jax.nn.gelu(u)
    v = jax.nn.gelu(v).reshape(B_, S // SGU_CHUNK, SGU_CHUNK, SGU_GROUPS, SGU_GROUP_DIM)
    v = rms_norm(v, sgu_gain.reshape(SGU_GROUPS, SGU_GROUP_DIM))
    causal = jnp.tril(jnp.ones((SGU_CHUNK, SGU_CHUNK), dtype=bool))
    w_causal = jnp.where(causal[None], w_s, jnp.zeros_like(w_s))
    mixed = jnp.einsum('gij,bcjgd->bcigd', w_causal, v) + b_s.T[None, None, :, :, None]
    b_out = u * mixed.reshape(B_, S, SGU_WIDTH)
    return jnp.concatenate([a_out, b_out], axis=-1) @ w_out


def moba_attention(q, k, v):
    B_, H, S, hd = q.shape
    s_pad = -(-S // MOBA_BLOCK) * MOBA_BLOCK
    pad = ((0, 0), (0, 0), (0, s_pad - S), (0, 0))
    q, k, v = jnp.pad(q, pad), jnp.pad(k, pad), jnp.pad(v, pad)
    nb = s_pad // MOBA_BLOCK
    topk = min(MOBA_TOPK, nb)
    k_blocks = k.reshape(B_, H, nb, MOBA_BLOCK, hd)
    v_blocks = v.reshape(B_, H, nb, MOBA_BLOCK, hd)
    k_mean = jnp.mean(k_blocks.astype(jnp.float32), axis=3)
    scale = HEAD_DIM ** -0.5
    slopes = alibi_slopes()
    b_ix = jnp.arange(B_)[:, None, None, None]
    h_ix = jnp.arange(H)[None, :, None, None]
    blk_off = jnp.arange(MOBA_BLOCK)

    def chunk(c):
        q0 = c * Q_CHUNK
        qc = lax.dynamic_slice_in_dim(q, q0, Q_CHUNK, axis=2)
        t = q0 + jnp.arange(Q_CHUNK)
        own = q0 // MOBA_BLOCK
        gate = jnp.einsum('bhqd,bhnd->bhqn', qc.astype(jnp.float32), k_mean)
        gate = jnp.where(jnp.arange(nb) < own, gate, -jnp.inf)
        _, idx = lax.top_k(gate, topk)
        valid = jnp.arange(topk) < own
        kg = k_blocks[b_ix, h_ix, idx]
        vg = v_blocks[b_ix, h_ix, idx]
        kpos = idx[..., None] * MOBA_BLOCK + blk_off
        dist_sel = (t[None, None, :, None, None] - kpos).astype(jnp.float32)
        s_sel = (jnp.einsum('bhqd,bhqjkd->bhqjk', qc, kg).astype(jnp.float32) * scale
                 - slopes[None, :, None, None, None] * dist_sel)
        s_sel = jnp.where(valid[:, None], s_sel, -jnp.inf).reshape(B_, H, Q_CHUNK, topk * MOBA_BLOCK)
        k_own = lax.dynamic_index_in_dim(k_blocks, own, axis=2, keepdims=False)
        v_own = lax.dynamic_index_in_dim(v_blocks, own, axis=2, keepdims=False)
        own_pos = own * MOBA_BLOCK + blk_off
        dist_own = (t[:, None] - own_pos[None, :]).astype(jnp.float32)
        s_own = (jnp.einsum('bhqd,bhkd->bhqk', qc, k_own).astype(jnp.float32) * scale
                 - slopes[None, :, None, None] * dist_own)
        s_own = jnp.where(own_pos[None, :] <= t[:, None], s_own, -jnp.inf)
        p = jax.nn.softmax(jnp.concatenate([s_sel, s_own], axis=-1), axis=-1)
        p_sel = p[..., :topk * MOBA_BLOCK].reshape(B_, H, Q_CHUNK, topk, MOBA_BLOCK).astype(v.dtype)
        p_own = p[..., topk * MOBA_BLOCK:].astype(v.dtype)
        return (jnp.einsum('bhqjk,bhqjkd->bhqd', p_sel, vg)
                + jnp.einsum('bhqk,bhkd->bhqd', p_own, v_own))

    outs = lax.map(chunk, jnp.arange(s_pad // Q_CHUNK))
    out = outs.transpose(1, 0, 3, 2, 4).reshape(B_, s_pad, H * hd)
    return out[:, :S]


def moba_mixer(h, w_qkv, q_gain, k_gain, w_o):
    B_, S, _ = h.shape
    qkv = (h @ w_qkv).reshape(B_, S, 3, N_HEADS, HEAD_DIM)
    q = rms_norm(qkv[:, :, 0], q_gain).transpose(0, 2, 1, 3)
    k = rms_norm(qkv[:, :, 1], k_gain).transpose(0, 2, 1, 3)
    v = qkv[:, :, 2].transpose(0, 2, 1, 3)
    return moba_attention(q, k, v) @ w_o


def channel_mlp(h, w_up, w_down):
    a = jax.nn.relu(h @ w_up)
    return (a * a) @ w_down


def setup_inputs(seed: int = 0) -> dict:
    key = jax.random.key(seed)
    ks = jax.random.split(key, 16)
    f32 = jnp.float32
    nrm = lambda k, shape, fan_in: jax.random.normal(k, shape, f32) * (fan_in ** -0.5)
    gain = lambda k, shape: 1.0 + 0.02 * jax.random.normal(k, shape, f32)
    return {
        "x": jax.random.normal(ks[0], (BATCH, SEQ, D_MODEL), f32),
        "mix_norm": gain(ks[1], (DEPTH, D_MODEL)),
        "ffn_norm": gain(ks[2], (DEPTH, D_MODEL)),
        "w_in": nrm(ks[3], (N_EVEN, D_MODEL, MIX_IN_WIDTH), D_MODEL),
        "conv_w": nrm(ks[4], (N_EVEN, CONV_WIDTH, CONV_K), CONV_K),
        "sgu_gain": gain(ks[5], (N_EVEN, SGU_WIDTH)),
        "w_s": nrm(ks[6], (N_EVEN, SGU_GROUPS, SGU_CHUNK, SGU_CHUNK), SGU_CHUNK),
        "b_s": 1.0 + 0.02 * jax.random.normal(ks[7], (N_EVEN, SGU_GROUPS, SGU_CHUNK), f32),
        "w_mix_out": nrm(ks[8], (N_EVEN, D_MODEL, D_MODEL), D_MODEL),
        "w_qkv": nrm(ks[9], (N_ODD, D_MODEL, 3 * D_MODEL), D_MODEL),
        "q_gain": gain(ks[10], (N_ODD, HEAD_DIM)),
        "k_gain": gain(ks[11], (N_ODD, HEAD_DIM)),
        "w_attn_out": nrm(ks[12], (N_ODD, D_MODEL, D_MODEL), D_MODEL),
        "w_up": nrm(ks[13], (DEPTH, D_MODEL, D_FF), D_MODEL),
        "w_down": nrm(ks[14], (DEPTH, D_FF, D_MODEL), D_FF),
    }


def reference(x, mix_norm, ffn_norm, w_in, conv_w, sgu_gain, w_s, b_s, w_mix_out,
              w_qkv, q_gain, k_gain, w_attn_out, w_up, w_down):
    for layer in range(DEPTH):
        i = layer // 2
        h = rms_norm(x, mix_norm[layer])
        if layer % 2 == 0:
            x = x + conv_sgu_mixer(h, w_in[i], conv_w[i], sgu_gain[i], w_s[i], b_s[i], w_mix_out[i])
        else:
            x = x + moba_mixer(h, w_qkv[i], q_gain[i], k_gain[i], w_attn_out[i])
        h = rms_norm(x, ffn_norm[layer])
        x = x + channel_mlp(h, w_up[layer], w_down[layer])
    return x
```

```python
import functools

import numpy as np
import jax
import jax.numpy as jnp
from jax import lax
from jax.experimental import pallas as pl
from jax.experimental.pallas import tpu as pltpu

EPS = 1e-6
SGU_GROUP_DIM = 128
SGU_CHUNK = 128
N_HEADS = 16
HEAD_DIM = 128
MOBA_BLOCK = 256
MOBA_TOPK = 3
MASK_COLS = 64
ALIBI_PIECES = 6
NEG_BIG = -1e30
VMEM_LIMIT = 56 * 1024 * 1024

F32 = jnp.float32
BF16 = jnp.bfloat16


def _rms(xf, g):
    ms = jnp.mean(xf * xf, axis=-1, keepdims=True)
    return xf * lax.rsqrt(ms + EPS) * g


def _params(n_axes):
    return pltpu.CompilerParams(
        dimension_semantics=("arbitrary",) * n_axes, vmem_limit_bytes=VMEM_LIMIT)


def _mix_in_kernel(x_ref, g_ref, w_ref, cw_ref, sg_ref, ws_ref, bs_ref,
                   a_ref, b_ref, h_scr, ybuf, carry, *, tm, tn, tiles_per_seq):
    i = pl.program_id(0)
    j = pl.program_id(1)

    @pl.when(j == 0)
    def _():
        h_scr[...] = _rms(x_ref[...], g_ref[...]).astype(BF16)

    z = jnp.dot(h_scr[...], w_ref[0], preferred_element_type=F32)
    gate_b = z[:, 0:tn]
    gate_c = z[:, tn:2 * tn]
    xa = z[:, 2 * tn:3 * tn]
    u = z[:, 3 * tn:4 * tn]
    v = z[:, 4 * tn:5 * tn]

    y = gate_c * xa
    prev = jnp.where(i % tiles_per_seq == 0, 0.0, carry[j])
    ybuf[0:8, :] = prev
    ybuf[8:8 + tm, :] = y
    carry[j] = y[tm - 8:tm, :]
    y1 = ybuf[7:7 + tm, :]
    y2 = ybuf[6:6 + tm, :]
    conv = y2 * cw_ref[0:1, :] + y1 * cw_ref[1:2, :] + y * cw_ref[2:3, :]
    a_ref[...] = (gate_b * conv).astype(BF16)

    ug = jax.nn.gelu(u)
    vg = jax.nn.gelu(v)
    row = lax.broadcasted_iota(jnp.int32, (SGU_CHUNK, SGU_CHUNK), 0)
    col = lax.broadcasted_iota(jnp.int32, (SGU_CHUNK, SGU_CHUNK), 1)
    for g in range(tn // SGU_GROUP_DIM):
        cs = slice(g * SGU_GROUP_DIM, (g + 1) * SGU_GROUP_DIM)
        vn = _rms(vg[:, cs], sg_ref[:, cs]).astype(BF16)
        wc = jnp.where(col <= row, ws_ref[g], 0.0).astype(BF16)
        for c in range(tm // SGU_CHUNK):
            rs = slice(c * SGU_CHUNK, (c + 1) * SGU_CHUNK)
            mixed = jnp.dot(wc, vn[rs, :], preferred_element_type=F32) + bs_ref[g]
            b_ref[rs, cs] = (ug[rs, cs] * mixed).astype(BF16)


def _mix_in(x2d, g, w_r, cw_t, sgu_gain, w_s, bs_b, *, seq, tm, tn):
    t, d = x2d.shape
    nj = w_r.shape[0]
    width = nj * tn
    gpb = tn // SGU_GROUP_DIM
    kern = functools.partial(_mix_in_kernel, tm=tm, tn=tn, tiles_per_seq=seq // tm)
    return pl.pallas_call(
        kern,
        grid=(t // tm, nj),
        in_specs=[
            pl.BlockSpec((tm, d), lambda i, j: (i, 0)),
            pl.BlockSpec((1, d), lambda i, j: (0, 0)),
            pl.BlockSpec((1, d, 5 * tn), lambda i, j: (j, 0, 0)),
            pl.BlockSpec((3, tn), lambda i, j: (0, j)),
            pl.BlockSpec((1, tn), lambda i, j: (0, j)),
            pl.BlockSpec((gpb, SGU_CHUNK, SGU_CHUNK), lambda i, j: (j, 0, 0)),
            pl.BlockSpec((gpb, SGU_CHUNK, SGU_GROUP_DIM), lambda i, j: (j, 0, 0)),
        ],
        out_specs=[
            pl.BlockSpec((tm, tn), lambda i, j: (i, j)),
            pl.BlockSpec((tm, tn), lambda i, j: (i, j)),
        ],
        out_shape=[jax.ShapeDtypeStruct((t, width), BF16),
                   jax.ShapeDtypeStruct((t, width), BF16)],
        scratch_shapes=[
            pltpu.VMEM((tm, d), BF16),
            pltpu.VMEM((tm + 8, tn), F32),
            pltpu.VMEM((nj, 8, tn), F32),
        ],
        compiler_params=_params(2),
        name="mix_in",
    )(x2d, g, w_r, cw_t, sgu_gain, w_s, bs_b)


def _proj_res_kernel(*refs, n_lhs, heads):
    lhs_refs = refs[:n_lhs]
    w_ref, x_ref, g_ref, xo_ref, ho_ref = refs[n_lhs:]
    if heads:
        pieces = [lhs_refs[0][0, hh] for hh in range(heads)]
    else:
        pieces = [r[...] for r in lhs_refs]
    lhs = jnp.concatenate(pieces, axis=1)
    xo = x_ref[...] + jnp.dot(lhs, w_ref[...], preferred_element_type=F32)
    xo_ref[...] = xo
    ho_ref[...] = _rms(xo, g_ref[...]).astype(BF16)


def _proj_res(lhs_list, w, x2d, g, *, tm, heads=0, seq=None):
    t, d = x2d.shape
    if heads:
        tps = seq // tm
        lhs_specs = [pl.BlockSpec((1, heads, tm, HEAD_DIM), lambda i: (i // tps, 0, i % tps, 0))]
    else:
        lhs_specs = [pl.BlockSpec((tm, a.shape[1]), lambda i: (i, 0)) for a in lhs_list]
    kern = functools.partial(_proj_res_kernel, n_lhs=len(lhs_list), heads=heads)
    return pl.pallas_call(
        kern,
        grid=(t // tm,),
        in_specs=lhs_specs + [
            pl.BlockSpec(w.shape, lambda i: (0, 0)),
            pl.BlockSpec((tm, d), lambda i: (i, 0)),
            pl.BlockSpec((1, d), lambda i: (0, 0)),
        ],
        out_specs=[pl.BlockSpec((tm, d), lambda i: (i, 0)),
                   pl.BlockSpec((tm, d), lambda i: (i, 0))],
        out_shape=[jax.ShapeDtypeStruct((t, d), F32),
                   jax.ShapeDtypeStruct((t, d), BF16)],
        compiler_params=_params(1),
        name="proj_res",
    )(*lhs_list, w, x2d, g)


def _mlp_kernel(h_ref, x_ref, wu_ref, wd_ref, o_ref):
    f = pl.program_id(1)

    @pl.when(f == 0)
    def _():
        o_ref[...] = x_ref[...]

    a = jnp.maximum(jnp.dot(h_ref[...], wu_ref[...], preferred_element_type=F32), 0.0)
    o_ref[...] += jnp.dot((a * a).astype(BF16), wd_ref[...], preferred_element_type=F32)


def _mlp(h, x2d, w_up, w_down, *, tm, tf):
    t, d = x2d.shape
    d_ff = w_up.shape[1]
    return pl.pallas_call(
        _mlp_kernel,
        grid=(t // tm, d_ff // tf),
        in_specs=[
            pl.BlockSpec((tm, d), lambda i, f: (i, 0)),
            pl.BlockSpec((tm, d), lambda i, f: (i, 0)),
            pl.BlockSpec((d, tf), lambda i, f: (0, f)),
            pl.BlockSpec((tf, d), lambda i, f: (f, 0)),
        ],
        out_specs=pl.BlockSpec((tm, d), lambda i, f: (i, 0)),
        out_shape=jax.ShapeDtypeStruct((t, d), F32),
        compiler_params=_params(2),
        name="mlp",
    )(h, x2d, w_up, w_down)


def _qkv_kernel(x_ref, g_ref, w_ref, qg_ref, kg_ref, q_ref, k_ref, v_ref, km_ref, h_scr,
                *, tm, hpb):
    j = pl.program_id(1)

    @pl.when(j == 0)
    def _():
        h_scr[...] = _rms(x_ref[...], g_ref[...]).astype(BF16)

    z = jnp.dot(h_scr[...], w_ref[0], preferred_element_type=F32)
    w = hpb * HEAD_DIM
    for hh in range(hpb):
        cs = slice(hh * HEAD_DIM, (hh + 1) * HEAD_DIM)
        q_ref[0, hh] = _rms(z[:, cs], qg_ref[...]).astype(BF16)
        kn = _rms(z[:, w + hh * HEAD_DIM:w + (hh + 1) * HEAD_DIM], kg_ref[...])
        k_ref[0, hh] = kn.astype(BF16)
        for blk in range(tm // MOBA_BLOCK):
            km_ref[0, hh, blk] = jnp.mean(
                kn[blk * MOBA_BLOCK:(blk + 1) * MOBA_BLOCK, :], axis=0, keepdims=True)
        v_ref[0, hh] = z[:, 2 * w + hh * HEAD_DIM:2 * w + (hh + 1) * HEAD_DIM].astype(BF16)


def _qkv(x2d, g, w_r, qg, kg, *, batch, seq, tm, hpb):
    t, d = x2d.shape
    nj = w_r.shape[0]
    tps = seq // tm
    nb = seq // MOBA_BLOCK
    bpt = tm // MOBA_BLOCK
    kern = functools.partial(_qkv_kernel, tm=tm, hpb=hpb)
    head_spec = pl.BlockSpec((1, hpb, tm, HEAD_DIM), lambda i, j: (i // tps, j, i % tps, 0))
    head_shape = jax.ShapeDtypeStruct((batch, N_HEADS, seq, HEAD_DIM), BF16)
    return pl.pallas_call(
        kern,
        grid=(t // tm, nj),
        in_specs=[
            pl.BlockSpec((tm, d), lambda i, j: (i, 0)),
            pl.BlockSpec((1, d), lambda i, j: (0, 0)),
            pl.BlockSpec((1, d, 3 * hpb * HEAD_DIM), lambda i, j: (j, 0, 0)),
            pl.BlockSpec((1, HEAD_DIM), lambda i, j: (0, 0)),
            pl.BlockSpec((1, HEAD_DIM), lambda i, j: (0, 0)),
        ],
        out_specs=[
            head_spec, head_spec, head_spec,
            pl.BlockSpec((1, hpb, bpt, 1, HEAD_DIM), lambda i, j: (i // tps, j, i % tps, 0, 0)),
        ],
        out_shape=[head_shape, head_shape, head_shape,
                   jax.ShapeDtypeStruct((batch, N_HEADS, nb, 1, HEAD_DIM), F32)],
        scratch_shapes=[pltpu.VMEM((tm, d), BF16)],
        compiler_params=_params(2),
        name="qkv",
    )(x2d, g, w_r, qg, kg)


def _moba_kernel(q_ref, k_ref, v_ref, km_ref, qx_ref, kx_ref, sl_ref, o_ref):
    i = pl.program_id(2)
    blk = MOBA_BLOCK
    q = q_ref[0, 0]
    lane = lax.broadcasted_iota(jnp.int32, (blk, 128), 1)
    lane_f = lane.astype(F32)

    gate = lax.dot_general(q, km_ref[0, 0], (((1,), (1,)), ((), ())),
                           preferred_element_type=F32)
    gate = jnp.where(lane < i, gate, -jnp.inf)
    sel = lane == i
    for r in range(MOBA_TOPK):
        mx = jnp.max(gate, axis=1, keepdims=True)
        idx = jnp.min(jnp.where(gate == mx, lane_f, 1e9), axis=1, keepdims=True)
        hit = lane_f == idx
        sel = sel | (hit & (r < i))
        gate = jnp.where(hit, -jnp.inf, gate)

    qx = jnp.where(lane < MASK_COLS, jnp.where(sel, 0.0, NEG_BIG), qx_ref[0].astype(F32))
    q2 = jnp.concatenate([q, qx.astype(BF16)], axis=1)
    kx_base = kx_ref[0]
    slope = sl_ref[0][:, 0:1]

    def scores(j):
        rows = pl.ds(pl.multiple_of(j * blk, blk), blk)
        kxj = jnp.where(lane == j, jnp.ones_like(kx_base), kx_base)
        k2 = jnp.concatenate([k_ref[0, 0, rows, :], kxj], axis=1)
        s = lax.dot_general(q2, k2, (((1,), (1,)), ((), ())), preferred_element_type=F32)
        return s, v_ref[0, 0, rows, :]

    s, vj = scores(i)
    r2 = lax.broadcasted_iota(jnp.int32, (blk, blk), 0)
    c2 = lax.broadcasted_iota(jnp.int32, (blk, blk), 1)
    s = jnp.where(c2 <= r2, s, NEG_BIG)
    m0 = jnp.max(s, axis=1, keepdims=True)
    p = jnp.exp(s - m0)
    l0 = jnp.sum(p, axis=1, keepdims=True)
    acc0 = jnp.dot(p.astype(BF16), vj, preferred_element_type=F32)

    def body(j, carry):
        m, l, acc = carry
        cij = slope * ((i - j) * blk).astype(F32)
        s, vj = scores(j)
        m_new = jnp.maximum(m, jnp.max(s, axis=1, keepdims=True) - cij)
        alpha = jnp.exp(m - m_new)
        p = jnp.exp(s - (m_new + cij))
        l = alpha * l + jnp.sum(p, axis=1, keepdims=True)
        acc = alpha * acc + jnp.dot(p.astype(BF16), vj, preferred_element_type=F32)
        return m_new, l, acc

    _, l, acc = lax.fori_loop(0, i, body, (m0, l0, acc0))
    o_ref[0, 0] = (acc / l).astype(BF16)


def _alibi_tables():
    blk = MOBA_BLOCK
    slopes = (2.0 ** (-8.0 * np.arange(1, N_HEADS + 1, dtype=np.float64) / N_HEADS)).astype(np.float32)
    mant, expo = np.frexp(slopes.astype(np.float64))
    mant_int = np.round(mant * 2.0 ** 24).astype(np.int64)
    pos = np.arange(blk)
    lo = (pos % 16).astype(np.float64)
    hi = (pos // 16 * 16).astype(np.float64)
    qx = np.zeros((N_HEADS, blk, 128), np.float64)
    kx = np.zeros((N_HEADS, blk, 128), np.float64)
    c0 = MASK_COLS
    c1 = MASK_COLS + 2 * ALIBI_PIECES
    for h in range(N_HEADS):
        for p in range(ALIBI_PIECES):
            piece = float((mant_int[h] >> (4 * p)) & 15) * 2.0 ** (float(expo[h]) - 24 + 4 * p)
            kx[h, :, c0 + 2 * p] = piece * lo
            kx[h, :, c0 + 2 * p + 1] = piece * hi
            qx[h, :, c0 + 2 * p] = 1.0
            qx[h, :, c0 + 2 * p + 1] = 1.0
            qx[h, :, c1 + 2 * p] = -piece * lo
            qx[h, :, c1 + 2 * p + 1] = -piece * hi
            kx[h, :, c1 + 2 * p] = 1.0
            kx[h, :, c1 + 2 * p + 1] = 1.0
    sl = np.broadcast_to(slopes[:, None, None], (N_HEADS, 1, 128)).astype(np.float32)
    return (jnp.asarray(qx, dtype=BF16), jnp.asarray(kx, dtype=BF16), jnp.asarray(sl))


def _moba(q, k, v, km_pad):
    batch, heads, seq, hd = q.shape
    blk = MOBA_BLOCK
    qx, kx, sl = _alibi_tables()
    tile = pl.BlockSpec((1, 1, blk, hd), lambda b, h, i: (b, h, i, 0))
    full = pl.BlockSpec((1, 1, seq, hd), lambda b, h, i: (b, h, 0, 0))
    return pl.pallas_call(
        _moba_kernel,
        grid=(batch, heads, seq // blk),
        in_specs=[
            tile, full, full,
            pl.BlockSpec((1, 1, 128, hd), lambda b, h, i: (b, h, 0, 0)),
            pl.BlockSpec((1, blk, 128), lambda b, h, i: (h, 0, 0)),
            pl.BlockSpec((1, blk, 128), lambda b, h, i: (h, 0, 0)),
            pl.BlockSpec((1, 1, 128), lambda b, h, i: (h, 0, 0)),
        ],
        out_specs=tile,
        out_shape=jax.ShapeDtypeStruct(q.shape, BF16),
        compiler_params=_params(3),
        name="moba",
    )(q, k, v, km_pad, qx, kx, sl)


def kernel(x, mix_norm, ffn_norm, w_in, conv_w, sgu_gain, w_s, b_s, w_mix_out,
           w_qkv, q_gain, k_gain, w_attn_out, w_up, w_down):
    batch, seq, d = x.shape
    t = batch * seq
    assert seq % 512 == 0 and seq // MOBA_BLOCK <= MASK_COLS
    assert d == N_HEADS * HEAD_DIM and mix_norm.shape[0] == 2
    x2d = x.reshape(t, d)

    tn = 256
    conv_width = conv_w.shape[1]
    nj = conv_width // tn
    w_r = w_in[0].reshape(d, 5, nj, tn).transpose(2, 0, 1, 3).reshape(nj, d, 5 * tn).astype(BF16)
    bs_b = jnp.broadcast_to(b_s[0][:, :, None], b_s.shape[1:] + (SGU_GROUP_DIM,))
    a_out, b_out = _mix_in(x2d, mix_norm[0][None], w_r, conv_w[0].T, sgu_gain[0][None],
                           w_s[0], bs_b, seq=seq, tm=512, tn=tn)
    x2d, h = _proj_res([a_out, b_out], w_mix_out[0].astype(BF16), x2d, ffn_norm[0][None], tm=256)
    x2d = _mlp(h, x2d, w_up[0].astype(BF16), w_down[0].astype(BF16), tm=512, tf=1024)

    hpb = 4
    njq = N_HEADS // hpb
    wq_r = (w_qkv[0].reshape(d, 3, njq, hpb * HEAD_DIM).transpose(2, 0, 1, 3)
            .reshape(njq, d, 3 * hpb * HEAD_DIM).astype(BF16))
    scale = HEAD_DIM ** -0.5
    q, k, v, km = _qkv(x2d, mix_norm[1][None], wq_r, (q_gain[0] * scale)[None], k_gain[0][None],
                       batch=batch, seq=seq, tm=512, hpb=hpb)
    nb = seq // MOBA_BLOCK
    km_pad = jnp.pad(km.reshape(batch, N_HEADS, nb, HEAD_DIM),
                     ((0, 0), (0, 0), (0, 128 - nb), (0, 0))).astype(BF16)
    attn = _moba(q, k, v, km_pad)
    x2d, h = _proj_res([attn], w_attn_out[0].astype(BF16), x2d, ffn_norm[1][None],
                       tm=256, heads=N_HEADS, seq=seq)
    x2d = _mlp(h, x2d, w_up[1].astype(BF16), w_down[1].astype(BF16), tm=512, tf=1024)
    return x2d.reshape(batch, seq, d)
```
